```python
import math
import jax
import jax.numpy as jnp
from jax import lax
import numpy as np

D_MODEL = 4096
BATCH = 32
SEQ = 256
DEPTH = 2
DEC_BATCH = 8
DEC_SEQ = 4096
PAST_LEN = 512

GRID_W = 64
NORM_EPS = 1e-6

SSM_D_INNER = D_MODEL
SSM_HEADDIM = 64
SSM_HEADS = SSM_D_INNER // SSM_HEADDIM
SSM_GROUPS = 8
SSM_HPG = SSM_HEADS // SSM_GROUPS
SSM_STATE = 128
SSM_CONV_K = 3
SSM_CHUNK = 128
XBC_DIM = SSM_D_INNER + 2 * SSM_GROUPS * SSM_STATE

DIFF_HD = 128
DIFF_VD = 2 * DIFF_HD
DIFF_HEADS = D_MODEL // DIFF_VD
QK_DIM = DIFF_HEADS * 2 * DIFF_HD
V_DIM = DIFF_HEADS * DIFF_VD
ATTN_BLOCK_Q = 128
ROPE_BASE = 10000.0

N_EXPERTS = 16
N_EXPERT_GROUPS = 4
EXPERTS_PER_GROUP = N_EXPERTS // N_EXPERT_GROUPS
TOP_K = 2
D_EXPERT = D_MODEL // 4

IN_SIZES = (SSM_D_INNER, XBC_DIM, SSM_HEADS, SSM_HEADS, QK_DIM, QK_DIM, V_DIM, D_MODEL, D_MODEL)
N_IN = sum(IN_SIZES)

kernel_name = "hybrid_ssd_diffattn_moe_dit_step"


def _split_points():
    return [int(p) for p in np.cumsum(IN_SIZES)[:-1]]


def rmsnorm(x, g):
    xf = x.astype(jnp.float32)
    y = xf * lax.rsqrt(jnp.mean(xf * xf, axis=-1, keepdims=True) + NORM_EPS)
    return (y * g.astype(jnp.float32)).astype(x.dtype)


def grouped_rmsnorm(y, g, groups):
    lead = y.shape[:-1]
    yg = y.astype(jnp.float32).reshape(*lead, groups, y.shape[-1] // groups)
    yg = yg * lax.rsqrt(jnp.mean(yg * yg, axis=-1, keepdims=True) + NORM_EPS)
    return yg.reshape(y.shape) * g.astype(jnp.float32)


def centred_dwconv(u, w, b):
    pad = w.shape[0] // 2
    out = lax.conv_general_dilated(
        u, w[:, None, :].astype(u.dtype), window_strides=(1,), padding=[(pad, pad)],
        dimension_numbers=("NWC", "WIO", "NWC"), feature_group_count=u.shape[-1])
    return out + b.astype(u.dtype)


def axial_rope_tables(n_tok):
    rows = n_tok // GRID_W
    row = jnp.repeat(jnp.arange(rows, dtype=jnp.float32), GRID_W)
    col = jnp.tile(jnp.arange(GRID_W, dtype=jnp.float32), rows)
    n_freq = DIFF_HD // 4
    inv = ROPE_BASE ** (-jnp.arange(n_freq, dtype=jnp.float32) / n_freq)
    ang = jnp.stack([row[:, None] * inv, col[:, None] * inv], axis=1)
    return jnp.cos(ang), jnp.sin(ang)


def apply_axial_rope(t, cos, sin):
    tr = t.astype(jnp.float32).reshape(*t.shape[:-1], 2, 2, DIFF_HD // 4)
    t1, t2 = tr[..., 0, :], tr[..., 1, :]
    c = cos[None, :, None, None]
    s = sin[None, :, None, None]
    out = jnp.stack([t1 * c - t2 * s, t2 * c + t1 * s], axis=-2)
    return out.reshape(t.shape).astype(t.dtype)


def ssd_scan(x, dt, a, bm, cm, h0):
    b, n = x.shape[:2]
    nc = n // SSM_CHUNK
    f32 = jnp.float32
    dt = dt.astype(f32)
    xg = (x.astype(f32) * dt[..., None]).reshape(b, nc, SSM_CHUNK, SSM_GROUPS, SSM_HPG, SSM_HEADDIM)
    ag = (dt * a.astype(f32)).reshape(b, nc, SSM_CHUNK, SSM_GROUPS, SSM_HPG)
    bg = bm.astype(f32).reshape(b, nc, SSM_CHUNK, SSM_GROUPS, SSM_STATE)
    cg = cm.astype(f32).reshape(b, nc, SSM_CHUNK, SSM_GROUPS, SSM_STATE)
    xg, ag, bg, cg = (jnp.moveaxis(t, 1, 0) for t in (xg, ag, bg, cg))
    causal = jnp.tril(jnp.ones((SSM_CHUNK, SSM_CHUNK), bool))[None, :, :, None, None]

    def step(h, inp):
        xc, ac, bc, cc = inp
        acs = jnp.cumsum(ac, axis=1)
        seg = acs[:, :, None] - acs[:, None, :]
        lmat = jnp.exp(jnp.where(causal, seg, -jnp.inf))
        cb = jnp.einsum("blgn,bsgn->blsg", cc, bc)
        y_diag = jnp.einsum("blsgr,bsgrp->blgrp", cb[..., None] * lmat, xc)
        y_off = jnp.einsum("blgn,bgrpn->blgrp", cc, h) * jnp.exp(acs)[..., None]
        decay_end = jnp.exp(acs[:, -1:] - acs)
        h_new = h * jnp.exp(acs[:, -1])[..., None, None] + jnp.einsum(
            "blgn,blgrp->bgrpn", bc, xc * decay_end[..., None])
        return h_new, y_diag + y_off

    h_init = h0.astype(f32).reshape(b, SSM_GROUPS, SSM_HPG, SSM_HEADDIM, SSM_STATE)
    h_fin, ys = lax.scan(step, h_init, (xg, ag, bg, cg))
    y = jnp.moveaxis(ys, 0, 1).reshape(b, n, SSM_HEADS, SSM_HEADDIM)
    return y, h_fin.reshape(b, SSM_HEADS, SSM_HEADDIM, SSM_STATE)


def diff_attention(q, k, v, lam):
    b, s = q.shape[:2]
    nb = s // ATTN_BLOCK_Q
    qb = jnp.moveaxis(q.reshape(b, nb, ATTN_BLOCK_Q, DIFF_HEADS, 2, DIFF_HD), 1, 0)
    k1, k2 = k[..., 0, :], k[..., 1, :]
    scale = DIFF_HD ** -0.5

    def block(qblk):
        s1 = jnp.einsum("bqhd,bkhd->bhqk", qblk[..., 0, :], k1, preferred_element_type=jnp.float32) * scale
        s2 = jnp.einsum("bqhd,bkhd->bhqk", qblk[..., 1, :], k2, preferred_element_type=jnp.float32) * scale
        p = jax.nn.softmax(s1, axis=-1) - lam * jax.nn.softmax(s2, axis=-1)
        return jnp.einsum("bhqk,bkhe->bqhe", p.astype(v.dtype), v)

    o = lax.map(block, qb)
    return jnp.moveaxis(o, 0, 1).reshape(b, s, DIFF_HEADS, DIFF_VD)


def routed_moe(h, w_router, b_router, w_gate, w_up, w_down):
    shape = h.shape
    ht = h.reshape(-1, shape[-1])
    n_tok = ht.shape[0]
    scores = jax.nn.sigmoid(ht.astype(jnp.float32) @ w_router.astype(jnp.float32))
    biased = (scores + b_router.astype(jnp.float32)).reshape(n_tok, N_EXPERT_GROUPS, EXPERTS_PER_GROUP)
    group_score = lax.top_k(biased, TOP_K)[0].sum(-1)
    best = jnp.argmax(group_score, axis=-1)
    in_group = best[:, None] == jnp.arange(N_EXPERT_GROUPS)
    masked = jnp.where(in_group[..., None], biased, -jnp.inf).reshape(n_tok, N_EXPERTS)
    _, idx = lax.top_k(masked, TOP_K)
    w_sel = jnp.take_along_axis(scores, idx, axis=-1)
    w_sel = w_sel / jnp.sum(w_sel, axis=-1, keepdims=True)
    combine = jnp.sum(jax.nn.one_hot(idx, N_EXPERTS, dtype=jnp.float32) * w_sel[..., None], axis=1)
    out = jnp.zeros(ht.shape, jnp.float32)
    for e in range(N_EXPERTS):
        act = jax.nn.silu(ht @ w_gate[e]) * (ht @ w_up[e])
        out = out + combine[:, e:e + 1] * (act @ w_down[e]).astype(jnp.float32)
    return out.astype(h.dtype).reshape(shape)


def mixing_sublayer(h, lp, layer, rope, ctx_k, ctx_v, h0f, h0b):
    b, n, _ = h.shape
    f32 = jnp.float32
    proj = h @ lp["w_in"]
    z, xbc, dtf, dtb, q, k, v, ga, gb = jnp.split(proj, _split_points(), axis=-1)

    xbc = jax.nn.silu(centred_dwconv(xbc, lp["conv_w"], lp["conv_b"]))
    xs, bm, cm = jnp.split(xbc, [SSM_D_INNER, SSM_D_INNER + SSM_GROUPS * SSM_STATE], axis=-1)
    xs = xs.reshape(b, n, SSM_HEADS, SSM_HEADDIM)
    bm = bm.reshape(b, n, SSM_GROUPS, SSM_STATE)
    cm = cm.reshape(b, n, SSM_GROUPS, SSM_STATE)
    dt_f = jax.nn.softplus((dtf + lp["dt_bias_f"]).astype(f32))
    dt_b = jax.nn.softplus((dtb + lp["dt_bias_b"]).astype(f32))
    a_f = -jnp.exp(lp["a_log_f"].astype(f32))
    a_b = -jnp.exp(lp["a_log_b"].astype(f32))
    y_f, hf = ssd_scan(xs, dt_f, a_f, bm, cm, h0f)
    y_b, hb = ssd_scan(xs[:, ::-1], dt_b[:, ::-1], a_b, bm[:, ::-1], cm[:, ::-1], h0b)
    y = y_f + y_b[:, ::-1] + lp["ssm_d"].astype(f32)[:, None] * xs.astype(f32)
    y = y.reshape(b, n, SSM_D_INNER) * jax.nn.silu(z.astype(f32))
    y = grouped_rmsnorm(y, lp["ssm_norm_g"], SSM_GROUPS).astype(h.dtype)
    branch_a = y @ lp["w_pa"]

    q = q.reshape(b, n, DIFF_HEADS, 2, DIFF_HD)
    k = k.reshape(b, n, DIFF_HEADS, 2, DIFF_HD)
    v = v.reshape(b, n, DIFF_HEADS, DIFF_VD)
    if rope is not None:
        q = apply_axial_rope(q, rope[0], rope[1])
        k = apply_axial_rope(k, rope[0], rope[1])
    if ctx_k is None:
        k_all, v_all = k, v
    else:
        k_all = jnp.concatenate([k, ctx_k.astype(k.dtype)], axis=1)
        v_all = jnp.concatenate([v, ctx_v.astype(v.dtype)], axis=1)
    lam_init = 0.8 - 0.6 * math.exp(-0.3 * layer)
    lq1, lk1, lq2, lk2 = lp["diff_lambda"].astype(f32)
    lam = jnp.exp(jnp.sum(lq1 * lk1)) - jnp.exp(jnp.sum(lq2 * lk2)) + lam_init
    o = diff_attention(q, k_all, v_all, lam)
    o = rmsnorm(o, lp["diff_subln_g"]) * (1.0 - lam_init)
    branch_b = o.reshape(b, n, V_DIM).astype(h.dtype) @ lp["w_pb"]

    merged = jax.nn.sigmoid(ga) * branch_a + jax.nn.sigmoid(gb) * branch_b
    return merged @ lp["w_o"], k, v, hf, hb


def trunk_layer(x, mod, lp, layer, rope, ctx_k, ctx_v, h0f, h0b, w_router, b_router):
    shift1, scale1, gate1, shift2, scale2, gate2 = jnp.split(mod, 6, axis=-1)
    h = rmsnorm(x, lp["norm1_g"]) * (1.0 + scale1) + shift1
    mix, k, v, hf, hb = mixing_sublayer(h, lp, layer, rope, ctx_k, ctx_v, h0f, h0b)
    x = x + gate1 * mix
    h = rmsnorm(x, lp["norm2_g"]) * (1.0 + scale2) + shift2
    x = x + gate2 * routed_moe(h, w_router, b_router, lp["w_e_gate"], lp["w_e_up"], lp["w_e_down"])
    return x, k, v, hf, hb


def setup_inputs(seed: int = 0) -> dict:
    key = jax.random.key(seed)
    keys = iter(jax.random.split(key, 40))

    def nrm(shape, scale):
        return jax.random.normal(next(keys), shape, jnp.float32) * scale

    def gain(shape):
        return 1.0 + nrm(shape, 0.02)

    def inv_softplus_dt():
        dt0 = jnp.exp(jax.random.uniform(next(keys), (DEPTH, SSM_HEADS), jnp.float32,
                                         math.log(1e-3), math.log(1e-1)))
        return dt0 + jnp.log(-jnp.expm1(-dt0))

    def a_log():
        return jnp.log(jax.random.uniform(next(keys), (DEPTH, SSM_HEADS), jnp.float32, 1.0, 16.0))

    return {
        "x_prompt": nrm((BATCH, SEQ, D_MODEL), 1.0),
        "x_sample": nrm((DEC_BATCH, DEC_SEQ, D_MODEL), 1.0),
        "cache_k": nrm((DEC_BATCH, DEPTH, PAST_LEN, DIFF_HEADS, 2, DIFF_HD), 1.0),
        "cache_v": nrm((DEC_BATCH, DEPTH, PAST_LEN, DIFF_HEADS, DIFF_VD), 1.0),
        "state_ssm_fwd": nrm((DEC_BATCH, DEPTH, SSM_HEADS, SSM_HEADDIM, SSM_STATE), 0.3),
        "state_ssm_bwd": nrm((DEC_BATCH, DEPTH, SSM_HEADS, SSM_HEADDIM, SSM_STATE), 0.3),
        "c": nrm((DEC_BATCH, D_MODEL), 1.0),
        "c_ctx": nrm((D_MODEL,), 1.0),
        "w_ada": nrm((DEPTH, D_MODEL, 6 * D_MODEL), 0.5 * D_MODEL ** -0.5),
        "b_ada": nrm((DEPTH, 6 * D_MODEL), 0.01),
        "norm1_g": gain((DEPTH, D_MODEL)),
        "norm2_g": gain((DEPTH, D_MODEL)),
        "w_in": nrm((DEPTH, D_MODEL, N_IN), D_MODEL ** -0.5),
        "conv_w": nrm((DEPTH, SSM_CONV_K, XBC_DIM), SSM_CONV_K ** -0.5),
        "conv_b": nrm((DEPTH, XBC_DIM), 0.01),
        "dt_bias_f": inv_softplus_dt(),
        "dt_bias_b": inv_softplus_dt(),
        "a_log_f": a_log(),
        "a_log_b": a_log(),
        "ssm_d": 1.0 + nrm((DEPTH, SSM_HEADS), 0.1),
        "ssm_norm_g": gain((DEPTH, SSM_D_INNER)),
        "w_pa": nrm((DEPTH, SSM_D_INNER, D_MODEL), SSM_D_INNER ** -0.5),
        "diff_lambda": nrm((DEPTH, 4, DIFF_HD), 0.1),
        "diff_subln_g": gain((DEPTH, DIFF_VD)),
        "w_pb": nrm((DEPTH, V_DIM, D_MODEL), V_DIM ** -0.5),
        "w_o": nrm((DEPTH, D_MODEL, D_MODEL), D_MODEL ** -0.5),
        "w_router": nrm((D_MODEL, N_EXPERTS), D_MODEL ** -0.5),
        "b_router": nrm((N_EXPERTS,), 0.01),
        "w_e_gate": nrm((DEPTH, N_EXPERTS, D_MODEL, D_EXPERT), D_MODEL ** -0.5),
        "w_e_up": nrm((DEPTH, N_EXPERTS, D_MODEL, D_EXPERT), D_MODEL ** -0.5),
        "w_e_down": nrm((DEPTH, N_EXPERTS, D_EXPERT, D_MODEL), D_EXPERT ** -0.5),
        "final_norm_g": gain((D_MODEL,)),
    }


def reference(x_prompt, x_sample, cache_k, cache_v, state_ssm_fwd, state_ssm_bwd, c, c_ctx,
              w_ada, b_ada, norm1_g, norm2_g, w_in, conv_w, conv_b, dt_bias_f, dt_bias_b,
              a_log_f, a_log_b, ssm_d, ssm_norm_g, w_pa, diff_lambda, diff_subln_g, w_pb, w_o,
              w_router, b_router, w_e_gate, w_e_up, w_e_down, final_norm_g):
    rope = axial_rope_tables(x_sample.shape[1])
    zero_state = jnp.zeros((x_prompt.shape[0], SSM_HEADS, SSM_HEADDIM, SSM_STATE), jnp.float32)
    xp, xs = x_prompt, x_sample
    ks, vs, hfs, hbs = [], [], [], []
    for layer in range(DEPTH):
        lp = {
            "norm1_g": norm1_g[layer], "norm2_g": norm2_g[layer], "w_in": w_in[layer],
            "conv_w": conv_w[layer], "conv_b": conv_b[layer],
            "dt_bias_f": dt_bias_f[layer], "dt_bias_b": dt_bias_b[layer],
            "a_log_f": a_log_f[layer], "a_log_b": a_log_b[layer], "ssm_d": ssm_d[layer],
            "ssm_norm_g": ssm_norm_g[layer], "w_pa": w_pa[layer],
            "diff_lambda": diff_lambda[layer], "diff_subln_g": diff_subln_g[layer],
            "w_pb": w_pb[layer], "w_o": w_o[layer],
            "w_e_gate": w_e_gate[layer], "w_e_up": w_e_up[layer], "w_e_down": w_e_down[layer],
        }
        mod_ctx = jax.nn.silu(c_ctx) @ w_ada[layer] + b_ada[layer]
        xp, k_l, v_l, hf_l, hb_l = trunk_layer(xp, mod_ctx, lp, layer, None, None, None,
                                               zero_state, zero_state, w_router, b_router)
        ks.append(k_l)
        vs.append(v_l)
        hfs.append(hf_l)
        hbs.append(hb_l)
        mod_lat = (jax.nn.silu(c) @ w_ada[layer] + b_ada[layer])[:, None, :]
        xs = trunk_layer(xs, mod_lat, lp, layer, rope, cache_k[:, layer], cache_v[:, layer],
                         state_ssm_fwd[:, layer], state_ssm_bwd[:, layer], w_router, b_router)[0]
    y_prompt = rmsnorm(xp, final_norm_g)
    y_sample = rmsnorm(xs, final_norm_g)
    new_cache_k = jnp.stack(ks, axis=1).astype(x_prompt.dtype)
    new_cache_v = jnp.stack(vs, axis=1).astype(x_prompt.dtype)
    new_state_ssm_fwd = jnp.stack(hfs, axis=1).astype(x_prompt.dtype)
    new_state_ssm_bwd = jnp.stack(hbs, axis=1).astype(x_prompt.dtype)
    return (y_prompt, y_sample, new_cache_k, new_cache_v, new_state_ssm_fwd, new_state_ssm_bwd)
```

```python
import functools
import math

import jax
import jax.numpy as jnp
from jax import lax
from jax.experimental import pallas as pl
from jax.experimental.pallas import tpu as pltpu

F32 = jnp.float32
BF16 = jnp.bfloat16
I32 = jnp.int32

NORM_EPS = 1e-6
GRID_W = 64
ROPE_BASE = 10000.0
SSM_CHUNK = 128
N_EXPERT_GROUPS = 4
LANES = 128
MOD_ROWS = 16
VMEM_LIMIT = 56 * 1024 * 1024

_NT = (((1,), (1,)), ((), ()))


def _cp(sem, vmem=VMEM_LIMIT):
    return pltpu.CompilerParams(dimension_semantics=sem, vmem_limit_bytes=vmem)


def _tile(n, prefs):
    for t in prefs:
        if n % t == 0:
            return t
    raise ValueError(f"no tile for {n} in {prefs}")


def _silu(x):
    return x * jax.nn.sigmoid(x)


def _ada_kernel(c_ref, w_ref, b_ref, o_ref):
    s = _silu(c_ref[...]).astype(BF16)
    o_ref[...] = jnp.dot(s, w_ref[...].astype(BF16), preferred_element_type=F32) + b_ref[...]


def _ada_mod(cvec, w_ada, b_ada):
    L, D, N6 = w_ada.shape
    tn = _tile(N6, (512, 256, 128))
    return pl.pallas_call(
        _ada_kernel,
        grid=(L, N6 // tn),
        in_specs=[
            pl.BlockSpec((MOD_ROWS, D), lambda l, j: (0, 0)),
            pl.BlockSpec((None, D, tn), lambda l, j: (l, 0, j)),
            pl.BlockSpec((None, 1, tn), lambda l, j: (l, 0, j)),
        ],
        out_specs=pl.BlockSpec((None, MOD_ROWS, tn), lambda l, j: (l, 0, j)),
        out_shape=jax.ShapeDtypeStruct((L, MOD_ROWS, N6), F32),
        name="ada_mod",
        compiler_params=_cp(("parallel", "parallel")),
    )(cvec, w_ada, b_ada.reshape(L, 1, N6))


class _Geo:
    def __init__(self, nbp, lp, nbs, ls):
        self.nbp, self.lp, self.nbs, self.ls = nbp, lp, nbs, ls
        self.P = nbp * lp
        self.S = nbs * ls
        self.T = self.P + self.S
        assert self.P % ls == 0 and lp % SSM_CHUNK == 0 and ls % SSM_CHUNK == 0
        assert nbs + 1 <= MOD_ROWS
        self.tr = _tile(math.gcd(lp, ls), (256, 128))

    def mod_row(self, i, tm):
        r0 = i * tm
        return jnp.where(r0 < self.P, self.nbs, jnp.maximum(r0 - self.P, 0) // self.ls)


def _mod_spec(geo, tm, k, width, col=None):
    if col is None:
        return pl.BlockSpec((None, None, 1, width), lambda i, *_: (geo.mod_row(i, tm), k, 0, 0))
    return pl.BlockSpec((None, None, 1, width), lambda i, j, *_: (geo.mod_row(i, tm), k, 0, j))


def _normmod(x, g, sh, sc):
    y = x * lax.rsqrt(jnp.mean(x * x, axis=-1, keepdims=True) + NORM_EPS)
    return (y * g) * (1.0 + sc) + sh


def _norm1_kernel(x_ref, g_ref, sh_ref, sc_ref, o_ref):
    o_ref[...] = _normmod(x_ref[...], g_ref[...], sh_ref[...], sc_ref[...]).astype(o_ref.dtype)


def _norm1(x, g, mod, geo):
    T, D = x.shape
    tm = geo.tr
    return pl.pallas_call(
        _norm1_kernel,
        grid=(T // tm,),
        in_specs=[
            pl.BlockSpec((tm, D), lambda i: (i, 0)),
            pl.BlockSpec((1, D), lambda i: (0, 0)),
            _mod_spec(geo, tm, 0, D),
            _mod_spec(geo, tm, 1, D),
        ],
        out_specs=pl.BlockSpec((tm, D), lambda i: (i, 0)),
        out_shape=jax.ShapeDtypeStruct((T, D), BF16),
        name="norm1",
        compiler_params=_cp(("parallel",)),
    )(x, g.reshape(1, D), mod, mod)


def _mm_kernel(a_ref, b_ref, o_ref):
    o_ref[...] = jnp.dot(a_ref[...], b_ref[...], preferred_element_type=F32).astype(o_ref.dtype)


def _matmul(a, b, out_dtype):
    M, K = a.shape
    N = b.shape[1]
    tm = _tile(M, (1024, 512, 256))
    tn = _tile(N, (512, 256, 128)) if N % 128 == 0 else N
    return pl.pallas_call(
        _mm_kernel,
        grid=(M // tm, N // tn),
        in_specs=[
            pl.BlockSpec((tm, K), lambda i, j: (i, 0)),
            pl.BlockSpec((K, tn), lambda i, j: (0, j)),
        ],
        out_specs=pl.BlockSpec((tm, tn), lambda i, j: (i, j)),
        out_shape=jax.ShapeDtypeStruct((M, N), out_dtype),
        name="in_proj",
        compiler_params=_cp(("parallel", "parallel")),
    )(a, b)


def _conv_kernel(u_ref, up_ref, un_ref, w_ref, b_ref, o_ref, *, tc, geo):
    row0 = pl.program_id(0) * tc
    in_p = row0 < geo.P
    pos = jnp.where(in_p, row0 % geo.lp, (row0 - geo.P) % geo.ls)
    seq_len = jnp.where(in_p, geo.lp, geo.ls)
    u = u_ref[...].astype(F32)
    hb = up_ref.shape[0]
    prev_row = jnp.where(pos == 0, 0.0, up_ref[...].astype(F32)[hb - 1:hb, :])
    next_row = jnp.where(pos + tc == seq_len, 0.0, un_ref[...].astype(F32)[0:1, :])
    rid = lax.broadcasted_iota(I32, u.shape, 0)
    um = jnp.where(rid == 0, prev_row, pltpu.roll(u, 1, 0))
    un = jnp.where(rid == tc - 1, next_row, pltpu.roll(u, tc - 1, 0))
    w = w_ref[...]
    y = um * w[0:1, :] + u * w[1:2, :] + un * w[2:3, :] + b_ref[...]
    o_ref[...] = _silu(y).astype(o_ref.dtype)


def _conv_silu(zx, conv_w, conv_b, geo, di):
    T = zx.shape[0]
    xbc = conv_w.shape[1]
    tc = geo.tr
    wc = _tile(math.gcd(xbc, di), (512, 256, 128))
    hb = 16
    c0 = di // wc
    nhb = T // hb
    return pl.pallas_call(
        functools.partial(_conv_kernel, tc=tc, geo=geo),
        grid=(T // tc, xbc // wc),
        in_specs=[
            pl.BlockSpec((tc, wc), lambda i, j: (i, c0 + j)),
            pl.BlockSpec((hb, wc), lambda i, j: (jnp.maximum(i * (tc // hb) - 1, 0), c0 + j)),
            pl.BlockSpec((hb, wc), lambda i, j: (jnp.minimum((i + 1) * (tc // hb), nhb - 1), c0 + j)),
            pl.BlockSpec((3, wc), lambda i, j: (0, j)),
            pl.BlockSpec((1, wc), lambda i, j: (0, j)),
        ],
        out_specs=pl.BlockSpec((tc, wc), lambda i, j: (i, j)),
        out_shape=jax.ShapeDtypeStruct((T, xbc), BF16),
        name="conv_silu",
        compiler_params=_cp(("parallel", "parallel")),
    )(zx, zx, zx, conv_w, conv_b.reshape(1, xbc))


def _ssd_kernel(x_ref, b_ref, c_ref, dt_ref, dtb_ref, alog_ref, h0_ref, y_ref, hfin_ref, st_ref,
                *, H, HD, N, G, reverse, geo):
    Q = SSM_CHUNK
    npblk, ncp, ncs = geo.P // Q, geo.lp // Q, geo.ls // Q
    nblk = geo.T // Q
    i = pl.program_id(0)
    blk = (nblk - 1 - i) if reverse else i
    in_p = blk < npblk
    cc = jnp.where(in_p, blk % ncp, (blk - npblk) % ncs)
    nc = jnp.where(in_p, ncp, ncs)
    first = (cc == nc - 1) if reverse else (cc == 0)
    last = (cc == 0) if reverse else (cc == nc - 1)
    DI = H * HD
    HPG = H // G
    GW = HPG * HD

    @pl.when(jnp.logical_and(first, in_p))
    def _():
        st_ref[...] = jnp.zeros_like(st_ref)

    @pl.when(jnp.logical_and(first, jnp.logical_not(in_p)))
    def _():
        for j in range(DI // LANES):
            st_ref[:, j * LANES:(j + 1) * LANES] = h0_ref[j * LANES:(j + 1) * LANES, :].T

    dt_all = dt_ref[...] + dtb_ref[...]
    dt_all = jnp.maximum(dt_all, 0.0) + jnp.log1p(jnp.exp(-jnp.abs(dt_all)))
    ac = dt_all * (-jnp.exp(alog_ref[...]))
    rid = lax.broadcasted_iota(I32, ac.shape, 0)
    gm = ac
    s = 1
    while s < Q:
        if reverse:
            gm = gm + jnp.where(rid < Q - s, pltpu.roll(gm, Q - s, 0), 0.0)
        else:
            gm = gm + jnp.where(rid >= s, pltpu.roll(gm, s, 0), 0.0)
        s *= 2
    gmt = gm.T
    off = H if reverse else 0
    r_end = 0 if reverse else Q - 1

    li = lax.broadcasted_iota(I32, (Q, Q), 0)
    si = lax.broadcasted_iota(I32, (Q, Q), 1)
    mask = (si >= li) if reverse else (li >= si)
    left = lax.broadcasted_iota(I32, (Q, 2 * HD), 1) < HD
    left_row = left[0:1, :]

    for g in range(G):
        bg = b_ref[:, g * N:(g + 1) * N]
        cg = c_ref[:, g * N:(g + 1) * N]
        cb = jnp.where(mask, lax.dot_general(cg, bg, _NT, preferred_element_type=F32), 0.0)
        bgt = bg.astype(F32).T.astype(BF16)
        sg = st_ref[:, g * GW:(g + 1) * GW]
        yoff = jnp.dot(cg, sg.astype(BF16), preferred_element_type=F32)
        for p in range(HPG // 2):
            h0i = g * HPG + 2 * p
            la = off + h0i
            ca, cbk = gm[:, la:la + 1], gm[:, la + 1:la + 2]
            ra, rb = gmt[la:la + 1, :], gmt[la + 1:la + 2, :]
            m = jnp.concatenate(
                [(cb * jnp.exp(jnp.minimum(ca - ra, 0.0))).astype(BF16),
                 (cb * jnp.exp(jnp.minimum(cbk - rb, 0.0))).astype(BF16)], axis=1)
            cols = slice(h0i * HD, (h0i + 2) * HD)
            xp = x_ref[:, cols].astype(F32)
            xdt = xp * jnp.where(left, dt_all[:, la:la + 1], dt_all[:, la + 1:la + 2])
            xbd = jnp.concatenate(
                [jnp.where(left, xdt, 0.0).astype(BF16), jnp.where(left, 0.0, xdt).astype(BF16)], axis=0)
            ydiag = jnp.dot(m, xbd, preferred_element_type=F32)
            cpair = jnp.where(left, ca, cbk)
            y_ref[:, cols] = ydiag + yoff[:, p * 2 * HD:(p + 1) * 2 * HD] * jnp.exp(cpair)
            gend = jnp.where(left_row, gm[r_end:r_end + 1, la:la + 1], gm[r_end:r_end + 1, la + 1:la + 2])
            xdec = (xdt * jnp.exp(gend - cpair)).astype(BF16)
            st_ref[:, cols] = (sg[:, p * 2 * HD:(p + 1) * 2 * HD] * jnp.exp(gend)
                               + jnp.dot(bgt, xdec, preferred_element_type=F32))

    @pl.when(jnp.logical_and(last, in_p))
    def _():
        for j in range(DI // LANES):
            hfin_ref[j * LANES:(j + 1) * LANES, :] = st_ref[:, j * LANES:(j + 1) * LANES].T


def _ssd(xbc_c, dt, dtb, alog, h0, geo, *, H, HD, N, G, reverse):
    T = xbc_c.shape[0]
    Q = SSM_CHUNK
    DI, GN = H * HD, G * N
    assert DI % GN == 0 and (H // G) % 2 == 0 and 2 * HD == LANES
    nblk, npblk = T // Q, geo.P // Q
    ncp, ncs = geo.lp // Q, geo.ls // Q

    def blk(i):
        return (nblk - 1 - i) if reverse else i

    def seq_s(i):
        return jnp.maximum(blk(i) - npblk, 0) // ncs

    def seq_p(i):
        return jnp.minimum(blk(i) // ncp, geo.nbp - 1)

    kern = functools.partial(_ssd_kernel, H=H, HD=HD, N=N, G=G, reverse=reverse, geo=geo)
    return pl.pallas_call(
        kern,
        grid=(nblk,),
        in_specs=[
            pl.BlockSpec((Q, DI), lambda i: (blk(i), 0)),
            pl.BlockSpec((Q, GN), lambda i: (blk(i), DI // GN)),
            pl.BlockSpec((Q, GN), lambda i: (blk(i), DI // GN + 1)),
            pl.BlockSpec((Q, 2 * H), lambda i: (blk(i), 0)),
            pl.BlockSpec((1, 2 * H), lambda i: (0, 0)),
            pl.BlockSpec((1, 2 * H), lambda i: (0, 0)),
            pl.BlockSpec((None, DI, N), lambda i: (seq_s(i), 0, 0)),
        ],
        out_specs=[
            pl.BlockSpec((Q, DI), lambda i: (blk(i), 0)),
            pl.BlockSpec((None, DI, N), lambda i: (seq_p(i), 0, 0)),
        ],
        out_shape=[jax.ShapeDtypeStruct((T, DI), F32),
                   jax.ShapeDtypeStruct((geo.nbp, DI, N), F32)],
        scratch_shapes=[pltpu.VMEM((N, DI), F32)],
        name="ssd_bwd" if reverse else "ssd_fwd",
        compiler_params=_cp(("arbitrary",)),
    )(xbc_c, xbc_c, xbc_c, dt, dtb, alog, h0)


def _ssdpost_kernel(yf_ref, yb_ref, xs_ref, z_ref, d_ref, g_ref, o_ref, *, G):
    z = z_ref[...].astype(F32)
    y = (yf_ref[...] + yb_ref[...] + d_ref[...] * xs_ref[...].astype(F32)) * _silu(z)
    gw = y.shape[1] // G
    for g in range(G):
        yg = y[:, g * gw:(g + 1) * gw]
        r = lax.rsqrt(jnp.mean(yg * yg, axis=-1, keepdims=True) + NORM_EPS)
        o_ref[:, g * gw:(g + 1) * gw] = (yg * r * g_ref[:, g * gw:(g + 1) * gw]).astype(o_ref.dtype)


def _ssd_post(yf, yb, xbc_c, zx, d_vec, gain, G):
    T, DI = yf.shape
    tm = _tile(T, (256, 128))
    row = lambda i: (i, 0)
    return pl.pallas_call(
        functools.partial(_ssdpost_kernel, G=G),
        grid=(T // tm,),
        in_specs=[pl.BlockSpec((tm, DI), row)] * 4 + [pl.BlockSpec((1, DI), lambda i: (0, 0))] * 2,
        out_specs=pl.BlockSpec((tm, DI), row),
        out_shape=jax.ShapeDtypeStruct((T, DI), BF16),
        name="ssd_post",
        compiler_params=_cp(("parallel",)),
    )(yf, yb, xbc_c, zx, d_vec.reshape(1, DI), gain.reshape(1, DI))


def _rope(t, cos, sin_signed):
    q4 = t.shape[1] // 4
    lane = lax.broadcasted_iota(I32, t.shape, 1)
    upper = jnp.bitwise_and(lane, q4) != 0
    partner = jnp.where(upper, pltpu.roll(t, q4, 1), pltpu.roll(t, t.shape[1] - q4, 1))
    return t * cos + partner * sin_signed


def _attn_kernel(*refs, tq, ls, past, hd, lam_init, use_rope, use_cache, kchunk):
    it = iter(refs)
    q_ref, k_ref, v_ref = next(it), next(it), next(it)
    ck_ref = cv_ref = cos_ref = sin_ref = None
    if use_cache:
        ck_ref, cv_ref = next(it), next(it)
    if use_rope:
        cos_ref, sin_ref = next(it), next(it)
    lam_ref, g_ref, o_ref, ks_ref, vs_ref = next(it), next(it), next(it), next(it), next(it)
    qi = pl.program_id(2)

    @pl.when(qi == 0)
    def _():
        def body(c, carry):
            rows = pl.ds(pl.multiple_of(c * kchunk, kchunk), kchunk)
            for m in range(2):
                km = k_ref[rows, m * hd:(m + 1) * hd]
                if use_rope:
                    km = _rope(km, cos_ref[rows, :], sin_ref[rows, :])
                ks_ref[rows, m * hd:(m + 1) * hd] = km.astype(BF16)
            vs_ref[rows, :] = v_ref[rows, :].astype(BF16)
            return carry
        lax.fori_loop(0, ls // kchunk, body, 0)
        if use_cache:
            ks_ref[ls:ls + past, :] = ck_ref[...].astype(BF16)
            vs_ref[ls:ls + past, :] = cv_ref[...].astype(BF16)

    lam4 = lam_ref[...]
    lam = (jnp.exp(jnp.sum(lam4[0:1, :] * lam4[1:2, :], axis=-1, keepdims=True))
           - jnp.exp(jnp.sum(lam4[2:3, :] * lam4[3:4, :], axis=-1, keepdims=True)) + lam_init)
    scale = hd ** -0.5
    qrows = pl.ds(pl.multiple_of(qi * tq, tq), tq)
    p = None
    for m in range(2):
        qm = q_ref[:, m * hd:(m + 1) * hd]
        if use_rope:
            qm = _rope(qm, cos_ref[qrows, :], sin_ref[qrows, :])
        qm = (qm * scale).astype(BF16)
        s = lax.dot_general(qm, ks_ref[:, m * hd:(m + 1) * hd], _NT, preferred_element_type=F32)
        e = jnp.exp(s - jnp.max(s, axis=-1, keepdims=True))
        den = jnp.sum(e, axis=-1, keepdims=True)
        if m == 0:
            p = e * (1.0 / den)
        else:
            p = p - e * (lam / den)
    o = jnp.dot(p.astype(BF16), vs_ref[...], preferred_element_type=F32)
    o = o * lax.rsqrt(jnp.mean(o * o, axis=-1, keepdims=True) + NORM_EPS)
    o_ref[...] = (o * g_ref[...] * (1.0 - lam_init)).astype(o_ref.dtype)


def _attention(qkv, o_prev, lam4, subln_g, lam_init, *, row0, nb, ls, nh, hd, tq,
               cache_k=None, cache_v=None, cos=None, sin=None):
    T = qkv.shape[0]
    vd = 2 * hd
    use_cache, use_rope = cache_k is not None, cos is not None
    past = cache_k.shape[1] if use_cache else 0
    nq = ls // tq
    assert row0 % ls == 0 and row0 % tq == 0
    qrow = lambda b, h, q: (row0 // tq + b * nq + q, h)
    in_specs = [
        pl.BlockSpec((tq, vd), qrow),
        pl.BlockSpec((ls, vd), lambda b, h, q: (row0 // ls + b, nh + h)),
        pl.BlockSpec((ls, vd), lambda b, h, q: (row0 // ls + b, 2 * nh + h)),
    ]
    args = [qkv, qkv, qkv]
    if use_cache:
        in_specs += [pl.BlockSpec((None, past, vd), lambda b, h, q: (b, 0, h))] * 2
        args += [cache_k, cache_v]
    if use_rope:
        in_specs += [pl.BlockSpec((ls, hd), lambda b, h, q: (0, 0))] * 2
        args += [cos, sin]
    in_specs += [pl.BlockSpec((4, hd), lambda b, h, q: (0, 0)),
                 pl.BlockSpec((1, vd), lambda b, h, q: (0, 0)),
                 pl.BlockSpec(memory_space=pl.ANY)]
    args += [lam4, subln_g.reshape(1, vd), o_prev]
    kern = functools.partial(_attn_kernel, tq=tq, ls=ls, past=past, hd=hd, lam_init=lam_init,
                             use_rope=use_rope, use_cache=use_cache, kchunk=_tile(ls, (512, 256)))

    def wrapped(*refs):
        kern(*refs[:len(args) - 1], *refs[len(args):])

    return pl.pallas_call(
        wrapped,
        grid=(nb, nh, nq),
        in_specs=in_specs,
        out_specs=pl.BlockSpec((tq, vd), qrow),
        out_shape=jax.ShapeDtypeStruct((T, nh * vd), BF16),
        scratch_shapes=[pltpu.VMEM((ls + past, vd), BF16), pltpu.VMEM((ls + past, vd), BF16)],
        input_output_aliases={len(args) - 1: 0},
        name="attn_latent" if use_rope else "attn_context",
        compiler_params=_cp(("parallel", "parallel", "arbitrary")),
    )(*args)


def _merge_kernel(ya_ref, ob_ref, wa_ref, wb_ref, ga_ref, gb_ref, o_ref):
    a = jnp.dot(ya_ref[...], wa_ref[...], preferred_element_type=F32)
    b = jnp.dot(ob_ref[...], wb_ref[...], preferred_element_type=F32)
    o_ref[...] = (jax.nn.sigmoid(ga_ref[...].astype(F32)) * a
                  + jax.nn.sigmoid(gb_ref[...].astype(F32)) * b).astype(o_ref.dtype)


def _merge(ya, ob, w_pa, w_pb, gates):
    T, D = ya.shape[0], w_pa.shape[1]
    tm = _tile(T, (512, 256))
    tn = _tile(D, (512, 256, 128))
    ng = D // tn
    return pl.pallas_call(
        _merge_kernel,
        grid=(T // tm, D // tn),
        in_specs=[
            pl.BlockSpec((tm, ya.shape[1]), lambda i, j: (i, 0)),
            pl.BlockSpec((tm, ob.shape[1]), lambda i, j: (i, 0)),
            pl.BlockSpec((w_pa.shape[0], tn), lambda i, j: (0, j)),
            pl.BlockSpec((w_pb.shape[0], tn), lambda i, j: (0, j)),
            pl.BlockSpec((tm, tn), lambda i, j: (i, j)),
            pl.BlockSpec((tm, tn), lambda i, j: (i, ng + j)),
        ],
        out_specs=pl.BlockSpec((tm, tn), lambda i, j: (i, j)),
        out_shape=jax.ShapeDtypeStruct((T, D), BF16),
        name="branch_merge",
        compiler_params=_cp(("parallel", "parallel")),
    )(ya, ob, w_pa, w_pb, gates, gates)


def _oproj_kernel(m_ref, w_ref, x_ref, gate_ref, o_ref):
    o_ref[...] = x_ref[...] + gate_ref[...] * jnp.dot(m_ref[...], w_ref[...], preferred_element_type=F32)


def _oproj(merged, w_o, x, mod, geo):
    T, D = x.shape
    tm = geo.tr
    tn = _tile(D, (512, 256, 128))
    return pl.pallas_call(
        _oproj_kernel,
        grid=(T // tm, D // tn),
        in_specs=[
            pl.BlockSpec((tm, D), lambda i, j: (i, 0)),
            pl.BlockSpec((D, tn), lambda i, j: (0, j)),
            pl.BlockSpec((tm, tn), lambda i, j: (i, j)),
            _mod_spec(geo, tm, 2, tn, col=True),
        ],
        out_specs=pl.BlockSpec((tm, tn), lambda i, j: (i, j)),
        out_shape=jax.ShapeDtypeStruct((T, D), F32),
        name="out_proj",
        compiler_params=_cp(("parallel", "parallel")),
    )(merged, w_o, x, mod)


def _norm2_kernel(x_ref, g_ref, sh_ref, sc_ref, wrh_ref, wrl_ref, br_ref, h_ref, ri_ref, rw_ref, cnt_ref,
                  carry_ref, *, E, NG):
    @pl.when(pl.program_id(0) == 0)
    def _():
        carry_ref[...] = jnp.zeros_like(carry_ref)

    h = _normmod(x_ref[...], g_ref[...], sh_ref[...], sc_ref[...])
    h_ref[...] = h
    tm = h.shape[0]
    hh = h.astype(BF16)
    hl = (h - hh.astype(F32)).astype(BF16)
    wh, wl = wrh_ref[...], wrl_ref[...]
    logits = (lax.dot_general(wh, hh, _NT, preferred_element_type=F32)
              + lax.dot_general(wl, hh, _NT, preferred_element_type=F32)
              + lax.dot_general(wh, hl, _NT, preferred_element_type=F32))
    scores = jax.nn.sigmoid(logits)
    biased = scores + br_ref[...]
    eid = lax.broadcasted_iota(I32, (E, tm), 0)
    grp = jnp.zeros_like(eid)
    for g in range(1, NG):
        grp = grp + (eid >= g * (E // NG)).astype(I32)
    neg = -jnp.inf

    def top1(v):
        mx = jnp.max(v, axis=0, keepdims=True)
        ix = jnp.min(jnp.where(v == mx, eid, E), axis=0, keepdims=True)
        return mx, ix

    best_v = best_g = None
    for g in range(NG):
        vg = jnp.where(grp == g, biased, neg)
        m1, i1 = top1(vg)
        m2, _ = top1(jnp.where(eid == i1, neg, vg))
        gs = m1 + m2
        if g == 0:
            best_v, best_g = gs, jnp.zeros_like(i1)
        else:
            upd = gs > best_v
            best_v = jnp.where(upd, gs, best_v)
            best_g = jnp.where(upd, g, best_g)
    masked = jnp.where(grp == best_g, biased, neg)
    _, i1 = top1(masked)
    _, i2 = top1(jnp.where(eid == i1, neg, masked))
    oh1, oh2 = eid == i1, eid == i2
    s1 = jnp.sum(jnp.where(oh1, scores, 0.0), axis=0, keepdims=True)
    s2 = jnp.sum(jnp.where(oh2, scores, 0.0), axis=0, keepdims=True)
    tot = s1 + s2
    rw_ref[0:1, :] = s1 / tot
    rw_ref[1:2, :] = s2 / tot

    oh = jnp.logical_or(oh1, oh2)
    ti = lax.broadcasted_iota(I32, (tm, tm), 0)
    tj = lax.broadcasted_iota(I32, (tm, tm), 1)
    before = (ti < tj).astype(BF16)
    cum = jnp.dot(oh.astype(BF16), before, preferred_element_type=F32) + carry_ref[:, 0:1]
    r1 = jnp.sum(jnp.where(oh1, cum, 0.0), axis=0, keepdims=True)
    r2 = jnp.sum(jnp.where(oh2, cum, 0.0), axis=0, keepdims=True)
    ri_ref[0:1, :] = i1
    ri_ref[1:2, :] = i2
    ri_ref[2:3, :] = r1.astype(I32)
    ri_ref[3:4, :] = r2.astype(I32)
    new_carry = carry_ref[...] + jnp.sum(oh.astype(F32), axis=1, keepdims=True)
    carry_ref[...] = new_carry
    cnt_ref[...] = new_carry


def _norm2_route(x, g, mod, wr_hi, wr_lo, b_router, geo):
    T, D = x.shape
    E = wr_hi.shape[0]
    tm = geo.tr
    full = lambda i: (0, 0)
    return pl.pallas_call(
        functools.partial(_norm2_kernel, E=E, NG=N_EXPERT_GROUPS),
        grid=(T // tm,),
        in_specs=[
            pl.BlockSpec((tm, D), lambda i: (i, 0)),
            pl.BlockSpec((1, D), full),
            _mod_spec(geo, tm, 3, D),
            _mod_spec(geo, tm, 4, D),
            pl.BlockSpec((E, D), full),
            pl.BlockSpec((E, D), full),
            pl.BlockSpec((E, 1), full),
        ],
        out_specs=[
            pl.BlockSpec((tm, D), lambda i: (i, 0)),
            pl.BlockSpec((4, tm), lambda i: (0, i)),
            pl.BlockSpec((2, tm), lambda i: (0, i)),
            pl.BlockSpec((E, LANES), full),
        ],
        out_shape=[
            jax.ShapeDtypeStruct((T, D), F32),
            jax.ShapeDtypeStruct((4, T), I32),
            jax.ShapeDtypeStruct((2, T), F32),
            jax.ShapeDtypeStruct((E, LANES), F32),
        ],
        scratch_shapes=[pltpu.VMEM((E, LANES), F32)],
        name="norm2_route",
        compiler_params=_cp(("arbitrary",)),
    )(x, g.reshape(1, D), mod, mod, wr_hi, wr_lo, b_router.reshape(E, 1))


def _dispatch_kernel(meta_ref, dest_ref, h_ref, xs_ref, zero_ref, sem, *, td, E, npad):
    def row_copy(r, slot):
        return pltpu.make_async_copy(h_ref.at[pl.ds(r, 1)], xs_ref.at[pl.ds(dest_ref[0, slot * td + r], 1)], sem)

    def issue(r, c):
        row_copy(r, 0).start()
        row_copy(r, 1).start()
        return c

    lax.fori_loop(0, td, issue, 0)

    def drain(r, c):
        row_copy(0, 0).wait()
        row_copy(0, 0).wait()
        return c

    lax.fori_loop(0, td, drain, 0)

    @pl.when(pl.program_id(0) == pl.num_programs(0) - 1)
    def _():
        zero_ref[...] = jnp.zeros_like(zero_ref)

        def zero_copy(row):
            return pltpu.make_async_copy(zero_ref.at[pl.ds(0, 1)], xs_ref.at[pl.ds(row, 1)], sem)

        def fill(lo, n):
            lax.fori_loop(0, n, lambda r, c: (zero_copy(lo + r).start(), c)[1], 0)
            lax.fori_loop(0, n, lambda r, c: (zero_copy(lo).wait(), c)[1], 0)

        for e in range(E):
            fill(meta_ref[e], meta_ref[E + e])
        fill(meta_ref[2 * E], npad - meta_ref[2 * E])


def _dispatch(h, dest_t, meta, npad, td, E):
    T, D = h.shape
    grid_spec = pltpu.PrefetchScalarGridSpec(
        num_scalar_prefetch=1,
        grid=(T // td,),
        in_specs=[
            pl.BlockSpec((None, 1, 2 * td), lambda i, m: (i, 0, 0), memory_space=pltpu.SMEM),
            pl.BlockSpec((td, D), lambda i, m: (i, 0)),
        ],
        out_specs=pl.BlockSpec(memory_space=pl.ANY),
        scratch_shapes=[pltpu.VMEM((8, D), F32), pltpu.SemaphoreType.DMA(())],
    )
    return pl.pallas_call(
        functools.partial(_dispatch_kernel, td=td, E=E, npad=npad),
        grid_spec=grid_spec,
        out_shape=jax.ShapeDtypeStruct((npad, D), F32),
        name="moe_dispatch",
        compiler_params=_cp(("arbitrary",)),
    )(meta, dest_t, h)


def _moe_up_kernel(te_ref, nv_ref, x_ref, wg_ref, wu_ref, a_ref):
    valid = pl.program_id(0) < nv_ref[0]

    @pl.when(valid)
    def _():
        x = x_ref[...].astype(BF16)
        g = jnp.dot(x, wg_ref[...], preferred_element_type=F32)
        u = jnp.dot(x, wu_ref[...], preferred_element_type=F32)
        a_ref[...] = (_silu(g) * u).astype(a_ref.dtype)

    @pl.when(jnp.logical_not(valid))
    def _():
        a_ref[...] = jnp.zeros_like(a_ref)


def _moe_down_kernel(te_ref, nv_ref, a_ref, wd_ref, y_ref):
    valid = pl.program_id(0) < nv_ref[0]

    @pl.when(valid)
    def _():
        y_ref[...] = jnp.dot(a_ref[...], wd_ref[...], preferred_element_type=F32)

    @pl.when(jnp.logical_not(valid))
    def _():
        y_ref[...] = jnp.zeros_like(y_ref)


def _moe_experts(xs, tile_expert, nvalid, w_gate, w_up, w_down, te):
    npad, D = xs.shape
    E, _, DE = w_gate.shape
    nt = npad // te
    row = lambda i, t, n: (jnp.minimum(i, n[0] - 1), 0)
    wsel = lambda i, t, n: (t[i], 0, 0)
    act = pl.pallas_call(
        _moe_up_kernel,
        grid_spec=pltpu.PrefetchScalarGridSpec(
            num_scalar_prefetch=2, grid=(nt,),
            in_specs=[pl.BlockSpec((te, D), row),
                      pl.BlockSpec((None, D, DE), wsel),
                      pl.BlockSpec((None, D, DE), wsel)],
            out_specs=pl.BlockSpec((te, DE), lambda i, t, n: (i, 0))),
        out_shape=jax.ShapeDtypeStruct((npad, DE), BF16),
        name="moe_gate_up",
        compiler_params=_cp(("arbitrary",)),
    )(tile_expert, nvalid, xs, w_gate, w_up)
    return pl.pallas_call(
        _moe_down_kernel,
        grid_spec=pltpu.PrefetchScalarGridSpec(
            num_scalar_prefetch=2, grid=(nt,),
            in_specs=[pl.BlockSpec((te, DE), row),
                      pl.BlockSpec((None, DE, D), wsel)],
            out_specs=pl.BlockSpec((te, D), lambda i, t, n: (i, 0))),
        out_shape=jax.ShapeDtypeStruct((npad, D), F32),
        name="moe_down",
        compiler_params=_cp(("arbitrary",)),
    )(tile_expert, nvalid, act, w_down)


def _combine_kernel(dest_ref, x_ref, w_ref, gate_ref, ys_ref, o_ref, buf_ref, sem, *, td):
    def row_copy(r, slot):
        return pltpu.make_async_copy(ys_ref.at[pl.ds(dest_ref[0, slot * td + r], 1)],
                                     buf_ref.at[slot, pl.ds(r, 1)], sem)

    def issue(r, c):
        row_copy(r, 0).start()
        row_copy(r, 1).start()
        return c

    lax.fori_loop(0, td, issue, 0)

    def drain(r, c):
        row_copy(0, 0).wait()
        row_copy(0, 0).wait()
        return c

    lax.fori_loop(0, td, drain, 0)
    w = w_ref[...]
    moe = w[:, 0:1] * buf_ref[0] + w[:, 1:2] * buf_ref[1]
    o_ref[...] = x_ref[...] + gate_ref[...] * moe


def _combine(x, ys, dest_t, w_t, mod, geo, td):
    T, D = x.shape
    return pl.pallas_call(
        functools.partial(_combine_kernel, td=td),
        grid=(T // td,),
        in_specs=[
            pl.BlockSpec((None, 1, 2 * td), lambda i: (i, 0, 0), memory_space=pltpu.SMEM),
            pl.BlockSpec((td, D), lambda i: (i, 0)),
            pl.BlockSpec((td, 2), lambda i: (i, 0)),
            _mod_spec(geo, td, 5, D),
            pl.BlockSpec(memory_space=pl.ANY),
        ],
        out_specs=pl.BlockSpec((td, D), lambda i: (i, 0)),
        out_shape=jax.ShapeDtypeStruct((T, D), F32),
        scratch_shapes=[pltpu.VMEM((2, td, D), F32), pltpu.SemaphoreType.DMA(())],
        name="moe_combine",
        compiler_params=_cp(("arbitrary",)),
    )(dest_t, x, w_t, mod, ys)


def _fnorm_kernel(x_ref, g_ref, o_ref):
    x = x_ref[...]
    o_ref[...] = x * lax.rsqrt(jnp.mean(x * x, axis=-1, keepdims=True) + NORM_EPS) * g_ref[...]


def _final_norm(x, g, row0, nrows, tm):
    D = x.shape[1]
    return pl.pallas_call(
        _fnorm_kernel,
        grid=(nrows // tm,),
        in_specs=[pl.BlockSpec((tm, D), lambda i: (row0 // tm + i, 0)),
                  pl.BlockSpec((1, D), lambda i: (0, 0))],
        out_specs=pl.BlockSpec((tm, D), lambda i: (i, 0)),
        out_shape=jax.ShapeDtypeStruct((nrows, D), F32),
        name="final_norm",
        compiler_params=_cp(("parallel",)),
    )(x, g.reshape(1, D))


def _rope_tables(n_tok, hd):
    rows = n_tok // GRID_W
    row = jnp.repeat(jnp.arange(rows, dtype=F32), GRID_W)
    col = jnp.tile(jnp.arange(GRID_W, dtype=F32), rows)
    nf = hd // 4
    inv = ROPE_BASE ** (-jnp.arange(nf, dtype=F32) / nf)
    ar, ac = row[:, None] * inv, col[:, None] * inv
    cos = jnp.concatenate([jnp.cos(ar), jnp.cos(ar), jnp.cos(ac), jnp.cos(ac)], axis=1)
    sin = jnp.concatenate([-jnp.sin(ar), jnp.sin(ar), -jnp.sin(ac), jnp.sin(ac)], axis=1)
    return cos, sin


def _route_plan(ri, cnt, te, npad, td):
    E = cnt.shape[0]
    T = ri.shape[1]
    cnt = cnt[:, 0].astype(I32)
    pad = ((cnt + te - 1) // te) * te
    ends = jnp.cumsum(pad)
    off = ends - pad
    dest = off[ri[0:2]] + ri[2:4]
    dest_t = dest.reshape(2, T // td, td).transpose(1, 0, 2).reshape(T // td, 1, 2 * td)
    tile_ends = ends // te
    nvalid = tile_ends[-1:]
    tiles = jnp.arange(npad // te, dtype=I32)
    tile_expert = jnp.sum((tiles[:, None] >= tile_ends[None, :]).astype(I32), axis=1)
    last_used = jnp.sum((nvalid - 1 >= tile_ends).astype(I32))
    tile_expert = jnp.minimum(tile_expert, last_used).astype(I32)
    meta = jnp.concatenate([off + cnt, pad - cnt, ends[-1:]]).astype(I32)
    return dest_t, tile_expert, nvalid.astype(I32), meta


def kernel(x_prompt, x_sample, cache_k, cache_v, state_ssm_fwd, state_ssm_bwd, c, c_ctx, w_ada, b_ada,
           norm1_g, norm2_g, w_in, conv_w, conv_b, dt_bias_f, dt_bias_b, a_log_f, a_log_b, ssm_d,
           ssm_norm_g, w_pa, diff_lambda, diff_subln_g, w_pb, w_o, w_router, b_router, w_e_gate,
           w_e_up, w_e_down, final_norm_g):
    nbp, lp, D = x_prompt.shape
    nbs, ls, _ = x_sample.shape
    L = w_in.shape[0]
    H, HD, N = state_ssm_fwd.shape[2:]
    DI = H * HD
    XBC = conv_w.shape[2]
    G = (XBC - DI) // (2 * N)
    nh, hd = cache_k.shape[3], cache_k.shape[5]
    vd = 2 * hd
    qk = nh * vd
    past = cache_k.shape[2]
    E = w_router.shape[1]
    geo = _Geo(nbp, lp, nbs, ls)
    P, T = geo.P, geo.T

    x = jnp.concatenate([x_prompt.reshape(P, D), x_sample.reshape(geo.S, D)], axis=0)
    cvec = jnp.zeros((MOD_ROWS, D), F32).at[:nbs].set(c).at[nbs].set(c_ctx)
    mod_all = _ada_mod(cvec, w_ada, b_ada).reshape(L, MOD_ROWS, 6, 1, D)
    cos, sin = _rope_tables(ls, hd)
    wr_t = w_router.T
    wr_hi = wr_t.astype(BF16)
    wr_lo = (wr_t - wr_hi.astype(F32)).astype(BF16)
    ck = cache_k.reshape(nbs, L, past, qk)
    cv = cache_v.reshape(nbs, L, past, qk)

    seg = [0, DI + XBC, DI + XBC + 2 * H, DI + XBC + 2 * H + 3 * qk, w_in.shape[2]]
    te = 256
    td = geo.tr
    npad = 2 * T + E * te
    ks, vs, hfs, hbs = [], [], [], []
    for l in range(L):
        mod = mod_all[l]
        w_l = w_in[l]
        h = _norm1(x, norm1_g[l], mod, geo)
        zx = _matmul(h, w_l[:, seg[0]:seg[1]].astype(BF16), BF16)
        dt = _matmul(h, w_l[:, seg[1]:seg[2]].astype(BF16), F32)
        qkv = _matmul(h, w_l[:, seg[2]:seg[3]].astype(BF16), F32)
        gates = _matmul(h, w_l[:, seg[3]:seg[4]].astype(BF16), BF16)

        xbc_c = _conv_silu(zx, conv_w[l], conv_b[l], geo, DI)
        dtb = jnp.concatenate([dt_bias_f[l], dt_bias_b[l]]).reshape(1, 2 * H)
        alog = jnp.concatenate([a_log_f[l], a_log_b[l]]).reshape(1, 2 * H)
        ssd = functools.partial(_ssd, xbc_c, dt, dtb, alog, geo=geo, H=H, HD=HD, N=N, G=G)
        yf, hf = ssd(state_ssm_fwd[:, l].reshape(nbs, DI, N), reverse=False)
        yb, hb = ssd(state_ssm_bwd[:, l].reshape(nbs, DI, N), reverse=True)
        ya = _ssd_post(yf, yb, xbc_c, zx, jnp.repeat(ssm_d[l], HD), ssm_norm_g[l], G)

        lam_init = 0.8 - 0.6 * math.exp(-0.3 * l)
        att = functools.partial(_attention, qkv, lam4=diff_lambda[l], subln_g=diff_subln_g[l],
                                lam_init=lam_init, nh=nh, hd=hd)
        ob = att(jnp.zeros((T, qk), BF16), row0=0, nb=nbp, ls=lp, tq=lp)
        ob = att(ob, row0=P, nb=nbs, ls=ls, tq=_tile(ls, (128,)),
                 cache_k=ck[:, l], cache_v=cv[:, l], cos=cos, sin=sin)

        merged = _merge(ya, ob, w_pa[l].astype(BF16), w_pb[l].astype(BF16), gates)
        x = _oproj(merged, w_o[l].astype(BF16), x, mod, geo)

        h2, ri, rw, cnt = _norm2_route(x, norm2_g[l], mod, wr_hi, wr_lo, b_router, geo)
        dest_t, tile_expert, nvalid, meta = _route_plan(ri, cnt, te, npad, td)
        xs = _dispatch(h2, dest_t, meta, npad, td, E)
        ys = _moe_experts(xs, tile_expert, nvalid, w_e_gate[l].astype(BF16), w_e_up[l].astype(BF16),
                          w_e_down[l].astype(BF16), te)
        x = _combine(x, ys, dest_t, rw.T, mod, geo, td)

        ks.append(qkv[:P, qk:2 * qk].reshape(nbp, lp, nh, 2, hd))
        vs.append(qkv[:P, 2 * qk:3 * qk].reshape(nbp, lp, nh, vd))
        hfs.append(hf.reshape(nbp, H, HD, N))
        hbs.append(hb.reshape(nbp, H, HD, N))

    tm = geo.tr
    y_prompt = _final_norm(x, final_norm_g, 0, P, tm).reshape(nbp, lp, D)
    y_sample = _final_norm(x, final_norm_g, P, geo.S, tm).reshape(nbs, ls, D)
    return (y_prompt, y_sample, jnp.stack(ks, axis=1), jnp.stack(vs, axis=1),
            jnp.stack(hfs, axis=1), jnp.stack(hbs, axis=1))
```
